```python
import jax, jax.numpy as jnp
from jax import lax
import numpy as np

D_MODEL = 2048
BATCH = 4
SEQ = 2048
DEPTH = 1

N_ATTN_HEADS = 8
HEAD_DIM = 128
ATTN_WIDTH = N_ATTN_HEADS * HEAD_DIM
POOL_WIDTH = D_MODEL - ATTN_WIDTH
POOL_WINDOWS = (2, 4, 8, 16)
N_POOL_GROUPS = len(POOL_WINDOWS)
POOL_GROUP_WIDTH = POOL_WIDTH // N_POOL_GROUPS
IN_WIDTH = 3 * ATTN_WIDTH + POOL_WIDTH
MOBA_BLOCK = 256
MOBA_TOPK = 3
Q_CHUNK = 32
ROPE_THETA = 10000.0
D_FF = -(-8 * D_MODEL // (3 * 256)) * 256
N_MOD = 6
EPS = 1e-6

kernel_name = "hymba_moba_pool_hybrid_layer"


def rmsnorm(x, g):
    xf = x.astype(jnp.float32)
    y = xf * lax.rsqrt(jnp.mean(xf * xf, axis=-1, keepdims=True) + EPS)
    return (y * g.astype(jnp.float32)).astype(x.dtype)


def rope(x, positions):
    dh = x.shape[-1]
    inv_freq = ROPE_THETA ** (-jnp.arange(0, dh, 2, dtype=jnp.float32) / dh)
    ang = positions[:, None, :, None].astype(jnp.float32) * inv_freq
    cos, sin = jnp.cos(ang), jnp.sin(ang)
    xf = x.astype(jnp.float32)
    x1, x2 = xf[..., : dh // 2], xf[..., dh // 2:]
    out = jnp.concatenate([x1 * cos - x2 * sin, x2 * cos + x1 * sin], axis=-1)
    return out.astype(x.dtype)


def moba_attention(q, k, v):
    B, H, S, Dh = q.shape
    nb = -(-S // MOBA_BLOCK)
    pad = nb * MOBA_BLOCK - S
    kb = jnp.pad(k, ((0, 0), (0, 0), (0, pad), (0, 0))).reshape(B, H, nb, MOBA_BLOCK, Dh)
    vb = jnp.pad(v, ((0, 0), (0, 0), (0, pad), (0, 0))).reshape(B, H, nb, MOBA_BLOCK, Dh)
    kmean = jnp.mean(kb.astype(jnp.float32), axis=3)
    kk = min(MOBA_TOPK, nb)
    scale = Dh ** -0.5
    bi = jnp.arange(B)[:, None, None, None]
    hi = jnp.arange(H)[None, :, None, None]
    blk_ids = jnp.arange(nb)

    def chunk(ci):
        start = ci * Q_CHUNK
        qc = lax.dynamic_slice_in_dim(q, start, Q_CHUNK, axis=2)
        t = start + jnp.arange(Q_CHUNK)
        own = start // MOBA_BLOCK
        gate = jnp.einsum('bhqd,bhnd->bhqn', qc.astype(jnp.float32), kmean)
        gate = jnp.where(blk_ids < own, gate, -jnp.inf)
        _, idx = lax.top_k(gate, kk)
        valid = idx < own
        ksel = kb[bi, hi, idx]
        vsel = vb[bi, hi, idx]
        s_sel = jnp.einsum('bhqd,bhqkjd->bhqkj', qc, ksel).astype(jnp.float32) * scale
        s_sel = jnp.where(valid[..., None], s_sel, -jnp.inf).reshape(B, H, Q_CHUNK, kk * MOBA_BLOCK)
        kown = lax.dynamic_index_in_dim(kb, own, axis=2, keepdims=False)
        vown = lax.dynamic_index_in_dim(vb, own, axis=2, keepdims=False)
        s_own = jnp.einsum('bhqd,bhjd->bhqj', qc, kown).astype(jnp.float32) * scale
        kpos = own * MOBA_BLOCK + jnp.arange(MOBA_BLOCK)
        s_own = jnp.where(kpos[None, :] <= t[:, None], s_own, -jnp.inf)
        p = jax.nn.softmax(jnp.concatenate([s_sel, s_own], axis=-1), axis=-1)
        p_sel = p[..., : kk * MOBA_BLOCK].reshape(B, H, Q_CHUNK, kk, MOBA_BLOCK).astype(v.dtype)
        p_own = p[..., kk * MOBA_BLOCK:].astype(v.dtype)
        return (jnp.einsum('bhqkj,bhqkjd->bhqd', p_sel, vsel)
                + jnp.einsum('bhqj,bhjd->bhqd', p_own, vown))

    out = lax.map(chunk, jnp.arange(S // Q_CHUNK))
    return out.transpose(1, 2, 0, 3, 4).reshape(B, H, S, Dh)


def pool_mixer(u, w_pool, pool_scale):
    B, S, _ = u.shape
    uf = u.astype(jnp.float32)
    cs = jnp.pad(jnp.cumsum(uf, axis=1), ((0, 0), (1, 0), (0, 0)))
    t = jnp.arange(S)
    outs = []
    for g, w in enumerate(POOL_WINDOWS):
        sl = slice(g * POOL_GROUP_WIDTH, (g + 1) * POOL_GROUP_WIDTH)
        lo = jnp.maximum(t + 1 - w, 0)
        win_sum = cs[:, 1:, sl] - cs[:, lo, sl]
        cnt = (t + 1 - lo).astype(jnp.float32)
        pooled = (win_sum / cnt[None, :, None] - uf[..., sl]).astype(u.dtype)
        outs.append(jnp.einsum('bsc,cd->bsd', pooled, w_pool[g]))
    return jnp.concatenate(outs, axis=-1) * pool_scale


def setup_inputs(seed: int = 0) -> dict:
    key = jax.random.key(seed)
    ks = jax.random.split(key, 18)
    f32 = jnp.float32
    nrm = lambda k, shape, s: jax.random.normal(k, shape, f32) * s
    x = jax.random.normal(ks[0], (BATCH, SEQ, D_MODEL), f32)
    c = jax.random.normal(ks[1], (BATCH, D_MODEL), f32)
    offs = jax.random.randint(ks[2], (BATCH, 1), 0, 1024, dtype=jnp.int32)
    positions = offs + jnp.arange(SEQ, dtype=jnp.int32)[None, :]
    return {
        "x": x,
        "c": c,
        "positions": positions,
        "w_ada": nrm(ks[3], (DEPTH, D_MODEL, N_MOD * D_MODEL), 0.5 * D_MODEL ** -0.5),
        "b_ada": nrm(ks[4], (DEPTH, N_MOD * D_MODEL), 0.02),
        "g_mix_norm": 1.0 + nrm(ks[5], (DEPTH, D_MODEL), 0.02),
        "w_in": nrm(ks[6], (DEPTH, D_MODEL, IN_WIDTH), D_MODEL ** -0.5),
        "g_q": 1.0 + nrm(ks[7], (DEPTH, HEAD_DIM), 0.02),
        "g_k": 1.0 + nrm(ks[8], (DEPTH, HEAD_DIM), 0.02),
        "w_pool": nrm(ks[9], (DEPTH, N_POOL_GROUPS, POOL_GROUP_WIDTH, POOL_GROUP_WIDTH), POOL_GROUP_WIDTH ** -0.5),
        "pool_scale": 1.0 + nrm(ks[10], (DEPTH, POOL_WIDTH), 0.1),
        "w_out": nrm(ks[11], (DEPTH, D_MODEL, D_MODEL), D_MODEL ** -0.5),
        "g_ffn_norm": 1.0 + nrm(ks[12], (DEPTH, D_MODEL), 0.02),
        "w_gate": nrm(ks[13], (DEPTH, D_MODEL, D_FF), D_MODEL ** -0.5),
        "w_up": nrm(ks[14], (DEPTH, D_MODEL, D_FF), D_MODEL ** -0.5),
        "w_down": nrm(ks[15], (DEPTH, D_FF, D_MODEL), D_FF ** -0.5),
    }


def reference(x, c, positions, w_ada, b_ada, g_mix_norm, w_in, g_q, g_k, w_pool,
              pool_scale, w_out, g_ffn_norm, w_gate, w_up, w_down):
    B, S, D = x.shape
    for l in range(DEPTH):
        mod = jnp.einsum('bd,de->be', jax.nn.silu(c), w_ada[l]) + b_ada[l]
        sh1, sc1, gt1, sh2, sc2, gt2 = [m[:, None, :] for m in jnp.split(mod, N_MOD, axis=-1)]

        h = rmsnorm(x, g_mix_norm[l]) * (1.0 + sc1) + sh1
        z = jnp.einsum('bsd,de->bse', h, w_in[l])
        q, k, v, u = jnp.split(z, [ATTN_WIDTH, 2 * ATTN_WIDTH, 3 * ATTN_WIDTH], axis=-1)
        to_heads = lambda a: a.reshape(B, S, N_ATTN_HEADS, HEAD_DIM)
        q = rope(rmsnorm(to_heads(q), g_q[l]).transpose(0, 2, 1, 3), positions)
        k = rope(rmsnorm(to_heads(k), g_k[l]).transpose(0, 2, 1, 3), positions)
        v = to_heads(v).transpose(0, 2, 1, 3)
        o_attn = moba_attention(q, k, v).transpose(0, 2, 1, 3).reshape(B, S, ATTN_WIDTH)
        o_pool = pool_mixer(u, w_pool[l], pool_scale[l])
        y = jnp.einsum('bse,ed->bsd', jnp.concatenate([o_attn, o_pool], axis=-1), w_out[l])
        x = x + gt1 * y

        h2 = rmsnorm(x, g_ffn_norm[l]) * (1.0 + sc2) + sh2
        a = jnp.einsum('bsd,df->bsf', h2, w_gate[l])
        b = jnp.einsum('bsd,df->bsf', h2, w_up[l])
        f = jnp.einsum('bsf,fd->bsd', jax.nn.silu(a) * b, w_down[l])
        x = x + gt2 * f
    return x
```

```python
import functools
import math

import jax
import jax.numpy as jnp
from jax import lax
from jax.experimental import pallas as pl
from jax.experimental.pallas import tpu as pltpu

F32 = jnp.float32
BF16 = jnp.bfloat16

N_ATTN_HEADS = 8
HEAD_DIM = 128
ATTN_WIDTH = N_ATTN_HEADS * HEAD_DIM
POOL_WINDOWS = (2, 4, 8, 16)
POOL_GROUP_WIDTH = 256
POOL_WIDTH = len(POOL_WINDOWS) * POOL_GROUP_WIDTH
MAX_POOL_WINDOW = max(POOL_WINDOWS)
MOBA_BLOCK = 256
MOBA_TOPK = 3
ROPE_THETA = 10000.0
N_MOD = 6
EPS = 1e-6
MASK_PENALTY = 1e30

MIB = 1024 * 1024
NT_DIMS = (((1,), (1,)), ((), ()))


def _rms_normalize(x):
    return x * lax.rsqrt(jnp.mean(x * x, axis=-1, keepdims=True) + EPS)


def _silu(a):
    return a * jax.nn.sigmoid(a)


def _ada_kernel(c_ref, w_ref, b_ref, o_ref):
    s = _silu(c_ref[...])
    o_ref[...] = jnp.dot(s.astype(BF16), w_ref[...].astype(BF16), preferred_element_type=F32) + b_ref[...]


def _ada(c_pad, w_ada, b_ada, tn):
    rows, d = c_pad.shape
    n_out = w_ada.shape[1]
    return pl.pallas_call(
        _ada_kernel,
        grid=(n_out // tn,),
        in_specs=[
            pl.BlockSpec((rows, d), lambda n: (0, 0)),
            pl.BlockSpec((d, tn), lambda n: (0, n)),
            pl.BlockSpec((1, tn), lambda n: (0, n)),
        ],
        out_specs=pl.BlockSpec((rows, tn), lambda n: (0, n)),
        out_shape=jax.ShapeDtypeStruct((rows, n_out), F32),
        compiler_params=pltpu.CompilerParams(
            dimension_semantics=("arbitrary",), vmem_limit_bytes=2 * 2 * d * tn * 4 + 8 * MIB),
        name="ada",
    )(c_pad, w_ada, b_ada)


def _in_proj_kernel(x_ref, sh_ref, sc_ref, g_ref, w_ref, gq_ref, gk_ref, pos_ref, freq_ref,
                    qkv_ref, u_ref, h_scr, cos_scr, sin_scr, *, n_qk_tiles, n_qkv_tiles, heads_per_tile):
    n = pl.program_id(1)

    @pl.when(n == 0)
    def _():
        y = _rms_normalize(x_ref[...]) * g_ref[...]
        h_scr[...] = (y * (1.0 + sc_ref[0]) + sh_ref[0]).astype(BF16)
        ang = pos_ref[...].astype(F32) * freq_ref[...]
        lane = lax.broadcasted_iota(jnp.int32, ang.shape, 1)
        cos_scr[...] = jnp.cos(ang)
        sin_scr[...] = jnp.where(lane < HEAD_DIM // 2, -jnp.sin(ang), jnp.sin(ang))

    z = jnp.dot(h_scr[...], w_ref[...], preferred_element_type=F32)

    @pl.when(n < n_qk_tiles)
    def _():
        g = jnp.where(n < n_qk_tiles // 2, gq_ref[...], gk_ref[...])
        cos = cos_scr[...]
        sin = sin_scr[...]
        for hh in range(heads_per_tile):
            cols = slice(hh * HEAD_DIM, (hh + 1) * HEAD_DIM)
            zn = _rms_normalize(z[:, cols]) * g
            qkv_ref[:, cols] = (zn * cos + pltpu.roll(zn, HEAD_DIM // 2, 1) * sin).astype(BF16)

    @pl.when((n >= n_qk_tiles) & (n < n_qkv_tiles))
    def _():
        qkv_ref[...] = z.astype(BF16)

    @pl.when(n >= n_qkv_tiles)
    def _():
        u_ref[...] = z


def _in_proj(x2d, mod3, g_mix, w_in, g_q, g_k, pos, inv_freq, *, seq, tm, tn):
    t, d = x2d.shape
    qkv_w = 3 * ATTN_WIDTH
    n_qkv_tiles = qkv_w // tn
    n_tiles = w_in.shape[1] // tn
    tiles_per_seq = seq // tm
    kern = functools.partial(_in_proj_kernel, n_qk_tiles=2 * ATTN_WIDTH // tn, n_qkv_tiles=n_qkv_tiles,
                             heads_per_tile=tn // HEAD_DIM)
    vmem = (2 * tm * d * 4 + tm * d * 2 + 2 * d * tn * 2 + 2 * tm * tn * 2 + 2 * tm * tn * 4
            + 2 * tm * HEAD_DIM * 4 + 2 * tm * 128 * 4 + 3 * tm * tn * 4)
    return pl.pallas_call(
        kern,
        grid=(t // tm, n_tiles),
        in_specs=[
            pl.BlockSpec((tm, d), lambda m, n: (m, 0)),
            pl.BlockSpec((1, 1, d), lambda m, n: ((m // tiles_per_seq) * N_MOD + 0, 0, 0)),
            pl.BlockSpec((1, 1, d), lambda m, n: ((m // tiles_per_seq) * N_MOD + 1, 0, 0)),
            pl.BlockSpec((1, d), lambda m, n: (0, 0)),
            pl.BlockSpec((d, tn), lambda m, n: (0, n)),
            pl.BlockSpec((1, HEAD_DIM), lambda m, n: (0, 0)),
            pl.BlockSpec((1, HEAD_DIM), lambda m, n: (0, 0)),
            pl.BlockSpec((tm, 1), lambda m, n: (m, 0)),
            pl.BlockSpec((1, HEAD_DIM), lambda m, n: (0, 0)),
        ],
        out_specs=[
            pl.BlockSpec((tm, tn), lambda m, n: (m, jnp.minimum(n, n_qkv_tiles - 1))),
            pl.BlockSpec((tm, tn), lambda m, n: (m, jnp.maximum(n - n_qkv_tiles, 0))),
        ],
        out_shape=[
            jax.ShapeDtypeStruct((t, qkv_w), BF16),
            jax.ShapeDtypeStruct((t, POOL_WIDTH), F32),
        ],
        scratch_shapes=[
            pltpu.VMEM((tm, d), BF16),
            pltpu.VMEM((tm, HEAD_DIM), F32),
            pltpu.VMEM((tm, HEAD_DIM), F32),
        ],
        compiler_params=pltpu.CompilerParams(
            dimension_semantics=("arbitrary", "arbitrary"), vmem_limit_bytes=vmem + 8 * MIB),
        name="in_proj",
    )(x2d, mod3, mod3, g_mix, w_in, g_q, g_k, pos, inv_freq)


def _attn_kernel(q_ref, k_ref, v_ref, o_ref, vt_scr, kmean_scr, pen_scr, *, n_blocks):
    blk = MOBA_BLOCK
    exp2_scale = (HEAD_DIM ** -0.5) * math.log2(math.e)

    for j in range(n_blocks):
        rows = slice(j * blk, (j + 1) * blk)
        kmean_scr[j:j + 1, :] = jnp.sum(k_ref[rows, :].astype(F32), axis=0, keepdims=True) * (1.0 / blk)
        vt_scr[j] = v_ref[rows, :].astype(F32).T.astype(BF16)
    km = kmean_scr[...]
    km_hi = km.astype(BF16)
    km_lo = (km - km_hi.astype(F32)).astype(BF16)
    km2 = jnp.concatenate([km_hi, km_lo], axis=0)

    blk_id = lax.broadcasted_iota(jnp.int32, (n_blocks, blk), 0)
    kpos = lax.broadcasted_iota(jnp.int32, (blk, blk), 0)
    qpos = lax.broadcasted_iota(jnp.int32, (blk, blk), 1)

    def q_block(i, carry):
        q_i = q_ref[pl.ds(pl.multiple_of(i * blk, blk), blk), :]

        g2 = lax.dot_general(km2, q_i, NT_DIMS, preferred_element_type=F32)
        gate = g2[:n_blocks] + g2[n_blocks:]
        past = blk_id < i
        for j in range(n_blocks):
            gj = gate[j:j + 1, :]
            beats = past & ((gate > gj) | ((gate == gj) & (blk_id < j)))
            rank = jnp.sum(beats.astype(F32), axis=0, keepdims=True)
            keep = (rank < MOBA_TOPK) & (j < i)
            pen_scr[j:j + 1, :] = jnp.where(keep, 0.0, MASK_PENALTY)

        k_i = k_ref[pl.ds(pl.multiple_of(i * blk, blk), blk), :]
        s = lax.dot_general(k_i, q_i, NT_DIMS, preferred_element_type=F32)
        s = jnp.where(kpos <= qpos, s, -MASK_PENALTY)
        m = jnp.max(s, axis=0, keepdims=True)
        p = jnp.exp2((s - m) * exp2_scale)
        l = jnp.sum(p, axis=0, keepdims=True)
        acc = jnp.dot(vt_scr[i], p.astype(BF16), preferred_element_type=F32)

        def past_block(j, mla):
            m, l, acc = mla
            k_j = k_ref[pl.ds(pl.multiple_of(j * blk, blk), blk), :]
            s = lax.dot_general(k_j, q_i, NT_DIMS, preferred_element_type=F32)
            pen = pen_scr[pl.ds(j, 1), :]
            m_new = jnp.maximum(m, jnp.max(s, axis=0, keepdims=True) - pen)
            alpha = jnp.exp2((m - m_new) * exp2_scale)
            p = jnp.exp2((s - (m_new + pen)) * exp2_scale)
            l = l * alpha + jnp.sum(p, axis=0, keepdims=True)
            acc = acc * alpha + jnp.dot(vt_scr[j], p.astype(BF16), preferred_element_type=F32)
            return m_new, l, acc

        m, l, acc = lax.fori_loop(0, i, past_block, (m, l, acc))
        o_ref[pl.ds(pl.multiple_of(i * blk, blk), blk), :] = (acc / l).T.astype(BF16)
        return carry

    lax.fori_loop(0, n_blocks, q_block, 0)


def _attention(qkv, *, batch, seq):
    t = qkv.shape[0]
    n_blocks = seq // MOBA_BLOCK
    h = N_ATTN_HEADS
    kern = functools.partial(_attn_kernel, n_blocks=n_blocks)
    return pl.pallas_call(
        kern,
        grid=(batch, h),
        in_specs=[
            pl.BlockSpec((seq, HEAD_DIM), lambda b, hh: (b, hh)),
            pl.BlockSpec((seq, HEAD_DIM), lambda b, hh: (b, h + hh)),
            pl.BlockSpec((seq, HEAD_DIM), lambda b, hh: (b, 2 * h + hh)),
        ],
        out_specs=pl.BlockSpec((seq, HEAD_DIM), lambda b, hh: (b, hh)),
        out_shape=jax.ShapeDtypeStruct((t, ATTN_WIDTH), BF16),
        scratch_shapes=[
            pltpu.VMEM((n_blocks, HEAD_DIM, MOBA_BLOCK), BF16),
            pltpu.VMEM((n_blocks, HEAD_DIM), F32),
            pltpu.VMEM((n_blocks, MOBA_BLOCK), F32),
        ],
        compiler_params=pltpu.CompilerParams(
            dimension_semantics=("arbitrary", "arbitrary"), vmem_limit_bytes=32 * MIB),
        name="moba_attn",
    )(qkv, qkv, qkv)


def _out_proj_kernel(oa_ref, u_ref, halo_ref, wp_ref, ps_ref, w_ref, x_ref, gt_ref, o_ref,
                     lhs_scr, ext_scr, *, seq, tm):
    m_idx = pl.program_id(0)
    n = pl.program_id(1)
    hw = MAX_POOL_WINDOW
    gw = POOL_GROUP_WIDTH

    @pl.when(n == 0)
    def _():
        lhs_scr[:, :ATTN_WIDTH] = oa_ref[...]
        t0 = (m_idx * tm) % seq
        halo = jnp.where(t0 == 0, 0.0, halo_ref[...])
        t = t0 + lax.broadcasted_iota(jnp.int32, (tm, 1), 0)
        for g, w in enumerate(POOL_WINDOWS):
            cols = slice(g * gw, (g + 1) * gw)
            ug = u_ref[:, cols]
            ext_scr[0:hw, :] = halo[:, cols]
            ext_scr[hw:, :] = ug
            d = 1
            while 2 * d < w:
                ext_scr[8:, :] = ext_scr[8:, :] + ext_scr[8 - d:hw + tm - d, :]
                d *= 2
            win = ext_scr[hw:, :] + ext_scr[hw - d:hw + tm - d, :]
            cnt = jnp.minimum(t + 1, w).astype(F32)
            pooled = win / cnt - ug
            og = jnp.dot(pooled.astype(BF16), wp_ref[g], preferred_element_type=F32) * ps_ref[:, cols]
            lhs_scr[:, ATTN_WIDTH + g * gw:ATTN_WIDTH + (g + 1) * gw] = og.astype(BF16)

    y = jnp.dot(lhs_scr[...], w_ref[...], preferred_element_type=F32)
    o_ref[...] = x_ref[...] + gt_ref[0] * y


def _out_proj(o_attn, u, w_pool, pool_scale, w_out, x2d, mod3, *, seq, tm, tn):
    t, d = x2d.shape
    hw = MAX_POOL_WINDOW
    tiles_per_seq = seq // tm
    kern = functools.partial(_out_proj_kernel, seq=seq, tm=tm)
    vmem = (2 * tm * ATTN_WIDTH * 2 + 2 * tm * POOL_WIDTH * 4 + 2 * d * tn * 2 + 4 * tm * tn * 4
            + tm * d * 2 + (tm + hw) * POOL_GROUP_WIDTH * 4 + 4 * tm * tn * 4)
    return pl.pallas_call(
        kern,
        grid=(t // tm, d // tn),
        in_specs=[
            pl.BlockSpec((tm, ATTN_WIDTH), lambda m, n: (m, 0)),
            pl.BlockSpec((tm, POOL_WIDTH), lambda m, n: (m, 0)),
            pl.BlockSpec((hw, POOL_WIDTH), lambda m, n: (jnp.maximum(m * (tm // hw) - 1, 0), 0)),
            pl.BlockSpec(w_pool.shape, lambda m, n: (0, 0, 0)),
            pl.BlockSpec((1, POOL_WIDTH), lambda m, n: (0, 0)),
            pl.BlockSpec((d, tn), lambda m, n: (0, n)),
            pl.BlockSpec((tm, tn), lambda m, n: (m, n)),
            pl.BlockSpec((1, 1, tn), lambda m, n: ((m // tiles_per_seq) * N_MOD + 2, 0, n)),
        ],
        out_specs=pl.BlockSpec((tm, tn), lambda m, n: (m, n)),
        out_shape=jax.ShapeDtypeStruct((t, d), F32),
        scratch_shapes=[
            pltpu.VMEM((tm, d), BF16),
            pltpu.VMEM((tm + hw, POOL_GROUP_WIDTH), F32),
        ],
        compiler_params=pltpu.CompilerParams(
            dimension_semantics=("arbitrary", "arbitrary"), vmem_limit_bytes=vmem + 8 * MIB),
        name="out_proj",
    )(o_attn, u, u, w_pool, pool_scale, w_out, x2d, mod3)


def _ffn_kernel(x_ref, sh_ref, sc_ref, gt_ref, g_ref, wg_ref, wu_ref, wd_ref, o_ref, h_scr):
    f = pl.program_id(1)

    @pl.when(f == 0)
    def _():
        x = x_ref[...]
        y = _rms_normalize(x) * g_ref[...]
        h_scr[...] = (y * (1.0 + sc_ref[0]) + sh_ref[0]).astype(BF16)
        o_ref[...] = x

    h = h_scr[...]
    a = jnp.dot(h, wg_ref[...], preferred_element_type=F32)
    b = jnp.dot(h, wu_ref[...], preferred_element_type=F32)
    act = (_silu(a) * b).astype(BF16)
    o_ref[...] += gt_ref[0] * jnp.dot(act, wd_ref[...], preferred_element_type=F32)


def _ffn(x1, mod3, g_ffn, w_gate, w_up, w_down, *, seq, tm, tf):
    t, d = x1.shape
    d_ff = w_gate.shape[1]
    tiles_per_seq = seq // tm
    mod_spec = lambda i: pl.BlockSpec((1, 1, d), lambda m, f: ((m // tiles_per_seq) * N_MOD + i, 0, 0))
    vmem = 4 * tm * d * 4 + tm * d * 2 + 2 * 3 * d * tf * 2 + 4 * tm * tf * 4 + tm * d * 4
    return pl.pallas_call(
        _ffn_kernel,
        grid=(t // tm, d_ff // tf),
        in_specs=[
            pl.BlockSpec((tm, d), lambda m, f: (m, 0)),
            mod_spec(3), mod_spec(4), mod_spec(5),
            pl.BlockSpec((1, d), lambda m, f: (0, 0)),
            pl.BlockSpec((d, tf), lambda m, f: (0, f)),
            pl.BlockSpec((d, tf), lambda m, f: (0, f)),
            pl.BlockSpec((tf, d), lambda m, f: (f, 0)),
        ],
        out_specs=pl.BlockSpec((tm, d), lambda m, f: (m, 0)),
        out_shape=jax.ShapeDtypeStruct((t, d), F32),
        scratch_shapes=[pltpu.VMEM((tm, d), BF16)],
        compiler_params=pltpu.CompilerParams(
            dimension_semantics=("arbitrary", "arbitrary"), vmem_limit_bytes=vmem + 8 * MIB),
        name="ffn",
    )(x1, mod3, mod3, mod3, g_ffn, w_gate, w_up, w_down)


def kernel(x, c, positions, w_ada, b_ada, g_mix_norm, w_in, g_q, g_k, w_pool, pool_scale, w_out,
           g_ffn_norm, w_gate, w_up, w_down):
    batch, seq, d = x.shape
    depth = w_ada.shape[0]
    t = batch * seq
    assert seq % MOBA_BLOCK == 0 and d == ATTN_WIDTH + POOL_WIDTH
    assert w_in.shape[2] == 3 * ATTN_WIDTH + POOL_WIDTH

    half = jnp.arange(0, HEAD_DIM, 2, dtype=F32)
    inv_freq = ROPE_THETA ** (-half / HEAD_DIM)
    inv_freq = jnp.concatenate([inv_freq, inv_freq]).reshape(1, HEAD_DIM)
    pos = positions.reshape(t, 1)
    c_pad = jnp.zeros((8, d), F32).at[:batch].set(c)

    x2d = x.reshape(t, d)
    for l in range(depth):
        mod = _ada(c_pad, w_ada[l], b_ada[l].reshape(1, -1), tn=1024)[:batch]
        mod3 = mod.reshape(batch * N_MOD, 1, d)
        qkv, u = _in_proj(x2d, mod3, g_mix_norm[l].reshape(1, d), w_in[l].astype(BF16),
                          g_q[l].reshape(1, HEAD_DIM), g_k[l].reshape(1, HEAD_DIM), pos, inv_freq,
                          seq=seq, tm=1024, tn=512)
        o_attn = _attention(qkv, batch=batch, seq=seq)
        x2d = _out_proj(o_attn, u, w_pool[l].astype(BF16), pool_scale[l].reshape(1, POOL_WIDTH),
                        w_out[l].astype(BF16), x2d, mod3, seq=seq, tm=512, tn=1024)
        x2d = _ffn(x2d, mod3, g_ffn_norm[l].reshape(1, d), w_gate[l].astype(BF16), w_up[l].astype(BF16),
                   w_down[l].astype(BF16), seq=seq, tm=512, tf=512)
    return x2d.reshape(batch, seq, d)
```

```python
import functools
import math

import jax
import jax.numpy as jnp
from jax import lax
from jax.experimental import pallas as pl
from jax.experimental.pallas import tpu as pltpu

F32 = jnp.float32
BF16 = jnp.bfloat16

N_ATTN_HEADS = 8
HEAD_DIM = 128
ATTN_WIDTH = N_ATTN_HEADS * HEAD_DIM
POOL_WINDOWS = (2, 4, 8, 16)
POOL_GROUP_WIDTH = 256
POOL_WIDTH = len(POOL_WINDOWS) * POOL_GROUP_WIDTH
MAX_POOL_WINDOW = max(POOL_WINDOWS)
MOBA_BLOCK = 256
MOBA_TOPK = 3
ROPE_THETA = 10000.0
N_MOD = 6
EPS = 1e-6
MASK_PENALTY = 1e30

MIB = 1024 * 1024
NT_DIMS = (((1,), (1,)), ((), ()))


def _rms_normalize(x):
    return x * lax.rsqrt(jnp.mean(x * x, axis=-1, keepdims=True) + EPS)


def _silu(a):
    return a * jax.nn.sigmoid(a)


def _ada_kernel(c_ref, w_ref, b_ref, o_ref):
    s = _silu(c_ref[...])
    o_ref[...] = jnp.dot(s.astype(BF16), w_ref[...].astype(BF16), preferred_element_type=F32) + b_ref[...]


def _ada(c_pad, w_ada, b_ada, tn):
    rows, d = c_pad.shape
    n_out = w_ada.shape[1]
    return pl.pallas_call(
        _ada_kernel,
        grid=(n_out // tn,),
        in_specs=[
            pl.BlockSpec((rows, d), lambda n: (0, 0)),
            pl.BlockSpec((d, tn), lambda n: (0, n)),
            pl.BlockSpec((1, tn), lambda n: (0, n)),
        ],
        out_specs=pl.BlockSpec((rows, tn), lambda n: (0, n)),
        out_shape=jax.ShapeDtypeStruct((rows, n_out), F32),
        compiler_params=pltpu.CompilerParams(
            dimension_semantics=("arbitrary",), vmem_limit_bytes=2 * 2 * d * tn * 4 + 8 * MIB),
        name="ada",
    )(c_pad, w_ada, b_ada)


def _in_proj_kernel(x_ref, sh_ref, sc_ref, g_ref, w_ref, gq_ref, gk_ref, pos_ref, freq_ref,
                    qkv_ref, u_ref, h_scr, cos_scr, sin_scr, *, n_qk_tiles, n_qkv_tiles, heads_per_tile):
    n = pl.program_id(1)

    @pl.when(n == 0)
    def _():
        y = _rms_normalize(x_ref[...]) * g_ref[...]
        h_scr[...] = (y * (1.0 + sc_ref[0]) + sh_ref[0]).astype(BF16)
        ang = pos_ref[...].astype(F32) * freq_ref[...]
        lane = lax.broadcasted_iota(jnp.int32, ang.shape, 1)
        cos_scr[...] = jnp.cos(ang)
        sin_scr[...] = jnp.where(lane < HEAD_DIM // 2, -jnp.sin(ang), jnp.sin(ang))

    z = jnp.dot(h_scr[...], w_ref[...], preferred_element_type=F32)

    @pl.when(n < n_qk_tiles)
    def _():
        g = jnp.where(n < n_qk_tiles // 2, gq_ref[...], gk_ref[...])
        cos = cos_scr[...]
        sin = sin_scr[...]
        for hh in range(heads_per_tile):
            cols = slice(hh * HEAD_DIM, (hh + 1) * HEAD_DIM)
            zn = _rms_normalize(z[:, cols]) * g
            qkv_ref[:, cols] = (zn * cos + pltpu.roll(zn, HEAD_DIM // 2, 1) * sin).astype(BF16)

    @pl.when((n >= n_qk_tiles) & (n < n_qkv_tiles))
    def _():
        qkv_ref[...] = z.astype(BF16)

    @pl.when(n >= n_qkv_tiles)
    def _():
        u_ref[...] = z


def _in_proj(x2d, mod3, g_mix, w_in, g_q, g_k, pos, inv_freq, *, seq, tm, tn):
    t, d = x2d.shape
    qkv_w = 3 * ATTN_WIDTH
    n_qkv_tiles = qkv_w // tn
    n_tiles = w_in.shape[1] // tn
    tiles_per_seq = seq // tm
    kern = functools.partial(_in_proj_kernel, n_qk_tiles=2 * ATTN_WIDTH // tn, n_qkv_tiles=n_qkv_tiles,
                             heads_per_tile=tn // HEAD_DIM)
    vmem = (2 * tm * d * 4 + tm * d * 2 + 2 * d * tn * 2 + 2 * tm * tn * 2 + 2 * tm * tn * 4
            + 2 * tm * HEAD_DIM * 4 + 2 * tm * 128 * 4 + 3 * tm * tn * 4)
    return pl.pallas_call(
        kern,
        grid=(t // tm, n_tiles),
        in_specs=[
            pl.BlockSpec((tm, d), lambda m, n: (m, 0)),
            pl.BlockSpec((1, 1, d), lambda m, n: ((m // tiles_per_seq) * N_MOD + 0, 0, 0)),
            pl.BlockSpec((1, 1, d), lambda m, n: ((m // tiles_per_seq) * N_MOD + 1, 0, 0)),
            pl.BlockSpec((1, d), lambda m, n: (0, 0)),
            pl.BlockSpec((d, tn), lambda m, n: (0, n)),
            pl.BlockSpec((1, HEAD_DIM), lambda m, n: (0, 0)),
            pl.BlockSpec((1, HEAD_DIM), lambda m, n: (0, 0)),
            pl.BlockSpec((tm, 1), lambda m, n: (m, 0)),
            pl.BlockSpec((1, HEAD_DIM), lambda m, n: (0, 0)),
        ],
        out_specs=[
            pl.BlockSpec((tm, tn), lambda m, n: (m, jnp.minimum(n, n_qkv_tiles - 1))),
            pl.BlockSpec((tm, tn), lambda m, n: (m, jnp.maximum(n - n_qkv_tiles, 0))),
        ],
        out_shape=[
            jax.ShapeDtypeStruct((t, qkv_w), BF16),
            jax.ShapeDtypeStruct((t, POOL_WIDTH), F32),
        ],
        scratch_shapes=[
            pltpu.VMEM((tm, d), BF16),
            pltpu.VMEM((tm, HEAD_DIM), F32),
            pltpu.VMEM((tm, HEAD_DIM), F32),
        ],
        compiler_params=pltpu.CompilerParams(
            dimension_semantics=("arbitrary", "arbitrary"), vmem_limit_bytes=vmem + 8 * MIB),
        name="in_proj",
    )(x2d, mod3, mod3, g_mix, w_in, g_q, g_k, pos, inv_freq)


def _attn_kernel(q_ref, k_ref, v_ref, o_ref, vt_scr, kmean_scr, cpen_scr, *, n_blocks):
    blk = MOBA_BLOCK
    c = (HEAD_DIM ** -0.5) * math.log2(math.e)

    for j in range(n_blocks):
        rows = slice(j * blk, (j + 1) * blk)
        kmean_scr[j:j + 1, :] = jnp.sum(k_ref[rows, :].astype(F32), axis=0, keepdims=True) * (1.0 / blk)
        vt_scr[:, rows] = v_ref[rows, :].astype(F32).T.astype(BF16)
    km = kmean_scr[...]
    km_hi = km.astype(BF16)
    km_lo = (km - km_hi.astype(F32)).astype(BF16)
    km2 = jnp.concatenate([km_hi, km_lo], axis=0)

    kpos = lax.broadcasted_iota(jnp.int32, (blk, blk), 0)
    qpos = lax.broadcasted_iota(jnp.int32, (blk, blk), 1)
    cpen_scr[...] = jnp.where(kpos <= qpos, 0.0, -MASK_PENALTY)

    qt = 2 * blk
    blk_id = lax.broadcasted_iota(jnp.int32, (n_blocks, qt), 0)
    upper_half = (lax.broadcasted_iota(jnp.int32, (n_blocks, qt), 1) >= blk).astype(jnp.int32)

    def scores(a):
        return lax.dot_general(k_ref[0:(a + 1) * qt, :], q_ref[a * qt:(a + 1) * qt, :], NT_DIMS,
                               preferred_element_type=F32)

    def finish(a, s_all):
        g2 = lax.dot_general(km2, q_ref[a * qt:(a + 1) * qt, :], NT_DIMS, preferred_element_type=F32)
        gate = g2[:n_blocks] + g2[n_blocks:]
        past = blk_id < 2 * a + upper_half
        pens = []
        for j in range(2 * a + 1):
            gj = gate[j:j + 1, :]
            beats = past & ((gate > gj) | ((gate == gj) & (blk_id < j)))
            rank = jnp.sum(beats.astype(F32), axis=0, keepdims=True)
            pens.append(jnp.where(rank < MOBA_TOPK, 0.0, MASK_PENALTY))

        p_cols, l_cols = [], []
        for half in range(2):
            own = 2 * a + half
            lanes = slice(half * blk, (half + 1) * blk)
            blocks = [s_all[j * blk:(j + 1) * blk, lanes] for j in range(own)]
            pen = [pens[j][:, lanes] for j in range(own)]
            s_own = s_all[own * blk:(own + 1) * blk, lanes] + cpen_scr[...]
            m = jnp.max(s_own, axis=0, keepdims=True)
            for j in range(own):
                m = jnp.maximum(m, jnp.max(blocks[j], axis=0, keepdims=True) - pen[j])
            mc = m * c
            ps = [jnp.exp2(blocks[j] * c - (mc + pen[j] * c)) for j in range(own)]
            ps.append(jnp.exp2(s_own * c - mc))
            l_cols.append(functools.reduce(lambda x, y: x + y, [jnp.sum(p, axis=0, keepdims=True) for p in ps]))
            ps = [p.astype(BF16) for p in ps]
            if half == 0:
                ps.append(jnp.zeros((blk, blk), BF16))
            p_cols.append(jnp.concatenate(ps, axis=0))
        p_all = jnp.concatenate(p_cols, axis=1)
        l = jnp.concatenate(l_cols, axis=1)
        acc = jnp.dot(vt_scr[:, 0:(a + 1) * qt], p_all, preferred_element_type=F32)
        o_ref[a * qt:(a + 1) * qt, :] = (acc / l).T.astype(BF16)

    n_tiles = n_blocks // 2
    s_next = scores(0)
    for a in range(n_tiles):
        s_cur = s_next
        if a + 1 < n_tiles:
            s_next = scores(a + 1)
        finish(a, s_cur)


def _attention(qkv, *, batch, seq):
    t = qkv.shape[0]
    n_blocks = seq // MOBA_BLOCK
    h = N_ATTN_HEADS
    kern = functools.partial(_attn_kernel, n_blocks=n_blocks)
    return pl.pallas_call(
        kern,
        grid=(batch, h),
        in_specs=[
            pl.BlockSpec((seq, HEAD_DIM), lambda b, hh: (b, hh)),
            pl.BlockSpec((seq, HEAD_DIM), lambda b, hh: (b, h + hh)),
            pl.BlockSpec((seq, HEAD_DIM), lambda b, hh: (b, 2 * h + hh)),
        ],
        out_specs=pl.BlockSpec((seq, HEAD_DIM), lambda b, hh: (b, hh)),
        out_shape=jax.ShapeDtypeStruct((t, ATTN_WIDTH), BF16),
        scratch_shapes=[
            pltpu.VMEM((HEAD_DIM, seq), BF16),
            pltpu.VMEM((n_blocks, HEAD_DIM), F32),
            pltpu.VMEM((MOBA_BLOCK, MOBA_BLOCK), F32),
        ],
        compiler_params=pltpu.CompilerParams(
            dimension_semantics=("arbitrary", "arbitrary"), vmem_limit_bytes=32 * MIB),
        name="moba_attn",
    )(qkv, qkv, qkv)


def _out_proj_kernel(oa_ref, u_ref, halo_ref, wp_ref, ps_ref, w_ref, x_ref, gt_ref, o_ref,
                     lhs_scr, ext_scr, *, seq, tm):
    m_idx = pl.program_id(0)
    n = pl.program_id(1)
    hw = MAX_POOL_WINDOW
    gw = POOL_GROUP_WIDTH

    @pl.when(n == 0)
    def _():
        lhs_scr[:, :ATTN_WIDTH] = oa_ref[...]
        t0 = (m_idx * tm) % seq
        halo = jnp.where(t0 == 0, 0.0, halo_ref[...])
        t = t0 + lax.broadcasted_iota(jnp.int32, (tm, 1), 0)
        for g, w in enumerate(POOL_WINDOWS):
            cols = slice(g * gw, (g + 1) * gw)
            ug = u_ref[:, cols]
            ext_scr[0:hw, :] = halo[:, cols]
            ext_scr[hw:, :] = ug
            d = 1
            while 2 * d < w:
                ext_scr[8:, :] = ext_scr[8:, :] + ext_scr[8 - d:hw + tm - d, :]
                d *= 2
            win = ext_scr[hw:, :] + ext_scr[hw - d:hw + tm - d, :]
            cnt = jnp.minimum(t + 1, w).astype(F32)
            pooled = win / cnt - ug
            og = jnp.dot(pooled.astype(BF16), wp_ref[g], preferred_element_type=F32) * ps_ref[:, cols]
            lhs_scr[:, ATTN_WIDTH + g * gw:ATTN_WIDTH + (g + 1) * gw] = og.astype(BF16)

    y = jnp.dot(lhs_scr[...], w_ref[...], preferred_element_type=F32)
    o_ref[...] = x_ref[...] + gt_ref[0] * y


def _out_proj(o_attn, u, w_pool, pool_scale, w_out, x2d, mod3, *, seq, tm, tn):
    t, d = x2d.shape
    hw = MAX_POOL_WINDOW
    tiles_per_seq = seq // tm
    kern = functools.partial(_out_proj_kernel, seq=seq, tm=tm)
    vmem = (2 * tm * ATTN_WIDTH * 2 + 2 * tm * POOL_WIDTH * 4 + 2 * d * tn * 2 + 4 * tm * tn * 4
            + tm * d * 2 + (tm + hw) * POOL_GROUP_WIDTH * 4 + 4 * tm * tn * 4)
    return pl.pallas_call(
        kern,
        grid=(t // tm, d // tn),
        in_specs=[
            pl.BlockSpec((tm, ATTN_WIDTH), lambda m, n: (m, 0)),
            pl.BlockSpec((tm, POOL_WIDTH), lambda m, n: (m, 0)),
            pl.BlockSpec((hw, POOL_WIDTH), lambda m, n: (jnp.maximum(m * (tm // hw) - 1, 0), 0)),
            pl.BlockSpec(w_pool.shape, lambda m, n: (0, 0, 0)),
            pl.BlockSpec((1, POOL_WIDTH), lambda m, n: (0, 0)),
            pl.BlockSpec((d, tn), lambda m, n: (0, n)),
            pl.BlockSpec((tm, tn), lambda m, n: (m, n)),
            pl.BlockSpec((1, 1, tn), lambda m, n: ((m // tiles_per_seq) * N_MOD + 2, 0, n)),
        ],
        out_specs=pl.BlockSpec((tm, tn), lambda m, n: (m, n)),
        out_shape=jax.ShapeDtypeStruct((t, d), F32),
        scratch_shapes=[
            pltpu.VMEM((tm, d), BF16),
            pltpu.VMEM((tm + hw, POOL_GROUP_WIDTH), F32),
        ],
        compiler_params=pltpu.CompilerParams(
            dimension_semantics=("arbitrary", "arbitrary"), vmem_limit_bytes=vmem + 8 * MIB),
        name="out_proj",
    )(o_attn, u, u, w_pool, pool_scale, w_out, x2d, mod3)


def _ffn_kernel(x_ref, sh_ref, sc_ref, gt_ref, g_ref, wg_ref, wu_ref, wd_ref, o_ref, h_scr):
    f = pl.program_id(1)

    @pl.when(f == 0)
    def _():
        x = x_ref[...]
        y = _rms_normalize(x) * g_ref[...]
        h_scr[...] = (y * (1.0 + sc_ref[0]) + sh_ref[0]).astype(BF16)
        o_ref[...] = x

    h = h_scr[...]
    a = jnp.dot(h, wg_ref[...], preferred_element_type=F32)
    b = jnp.dot(h, wu_ref[...], preferred_element_type=F32)
    act = (_silu(a) * b).astype(BF16)
    o_ref[...] += gt_ref[0] * jnp.dot(act, wd_ref[...], preferred_element_type=F32)


def _ffn(x1, mod3, g_ffn, w_gate, w_up, w_down, *, seq, tm, tf):
    t, d = x1.shape
    d_ff = w_gate.shape[1]
    tiles_per_seq = seq // tm
    mod_spec = lambda i: pl.BlockSpec((1, 1, d), lambda m, f: ((m // tiles_per_seq) * N_MOD + i, 0, 0))
    vmem = 4 * tm * d * 4 + tm * d * 2 + 2 * 3 * d * tf * 2 + 4 * tm * tf * 4 + tm * d * 4
    return pl.pallas_call(
        _ffn_kernel,
        grid=(t // tm, d_ff // tf),
        in_specs=[
            pl.BlockSpec((tm, d), lambda m, f: (m, 0)),
            mod_spec(3), mod_spec(4), mod_spec(5),
            pl.BlockSpec((1, d), lambda m, f: (0, 0)),
            pl.BlockSpec((d, tf), lambda m, f: (0, f)),
            pl.BlockSpec((d, tf), lambda m, f: (0, f)),
            pl.BlockSpec((tf, d), lambda m, f: (f, 0)),
        ],
        out_specs=pl.BlockSpec((tm, d), lambda m, f: (m, 0)),
        out_shape=jax.ShapeDtypeStruct((t, d), F32),
        scratch_shapes=[pltpu.VMEM((tm, d), BF16)],
        compiler_params=pltpu.CompilerParams(
            dimension_semantics=("arbitrary", "arbitrary"), vmem_limit_bytes=vmem + 8 * MIB),
        name="ffn",
    )(x1, mod3, mod3, mod3, g_ffn, w_gate, w_up, w_down)


def kernel(x, c, positions, w_ada, b_ada, g_mix_norm, w_in, g_q, g_k, w_pool, pool_scale, w_out,
           g_ffn_norm, w_gate, w_up, w_down):
    batch, seq, d = x.shape
    depth = w_ada.shape[0]
    t = batch * seq
    assert seq % (2 * MOBA_BLOCK) == 0 and d == ATTN_WIDTH + POOL_WIDTH
    assert w_in.shape[2] == 3 * ATTN_WIDTH + POOL_WIDTH

    half = jnp.arange(0, HEAD_DIM, 2, dtype=F32)
    inv_freq = ROPE_THETA ** (-half / HEAD_DIM)
    inv_freq = jnp.concatenate([inv_freq, inv_freq]).reshape(1, HEAD_DIM)
    pos = positions.reshape(t, 1)
    c_pad = jnp.zeros((8, d), F32).at[:batch].set(c)

    x2d = x.reshape(t, d)
    for l in range(depth):
        mod = _ada(c_pad, w_ada[l], b_ada[l].reshape(1, -1), tn=1024)[:batch]
        mod3 = mod.reshape(batch * N_MOD, 1, d)
        qkv, u = _in_proj(x2d, mod3, g_mix_norm[l].reshape(1, d), w_in[l].astype(BF16),
                          g_q[l].reshape(1, HEAD_DIM), g_k[l].reshape(1, HEAD_DIM), pos, inv_freq,
                          seq=seq, tm=1024, tn=512)
        o_attn = _attention(qkv, batch=batch, seq=seq)
        x2d = _out_proj(o_attn, u, w_pool[l].astype(BF16), pool_scale[l].reshape(1, POOL_WIDTH),
                        w_out[l].astype(BF16), x2d, mod3, seq=seq, tm=512, tn=1024)
        x2d = _ffn(x2d, mod3, g_ffn_norm[l].reshape(1, d), w_gate[l].astype(BF16), w_up[l].astype(BF16),
                   w_down[l].astype(BF16), seq=seq, tm=512, tf=512)
    return x2d.reshape(batch, seq, d)
```

```python
import functools
import math

import jax
import jax.numpy as jnp
from jax import lax
from jax.experimental import pallas as pl
from jax.experimental.pallas import tpu as pltpu

F32 = jnp.float32
BF16 = jnp.bfloat16

N_ATTN_HEADS = 8
HEAD_DIM = 128
ATTN_WIDTH = N_ATTN_HEADS * HEAD_DIM
POOL_WINDOWS = (2, 4, 8, 16)
POOL_GROUP_WIDTH = 256
POOL_WIDTH = len(POOL_WINDOWS) * POOL_GROUP_WIDTH
MAX_POOL_WINDOW = max(POOL_WINDOWS)
MOBA_BLOCK = 256
MOBA_TOPK = 3
ROPE_THETA = 10000.0
N_MOD = 6
EPS = 1e-6
MASK_PENALTY = 1e30

MIB = 1024 * 1024
NT_DIMS = (((1,), (1,)), ((), ()))


def _rms_normalize(x):
    return x * lax.rsqrt(jnp.mean(x * x, axis=-1, keepdims=True) + EPS)


def _silu(a):
    return a * jax.nn.sigmoid(a)


def _ada_kernel(c_ref, w_ref, b_ref, o_ref):
    s = _silu(c_ref[...])
    o_ref[...] = jnp.dot(s.astype(BF16), w_ref[...].astype(BF16), preferred_element_type=F32) + b_ref[...]


def _ada(c_pad, w_ada, b_ada, tn):
    rows, d = c_pad.shape
    n_out = w_ada.shape[1]
    return pl.pallas_call(
        _ada_kernel,
        grid=(n_out // tn,),
        in_specs=[
            pl.BlockSpec((rows, d), lambda n: (0, 0)),
            pl.BlockSpec((d, tn), lambda n: (0, n)),
            pl.BlockSpec((1, tn), lambda n: (0, n)),
        ],
        out_specs=pl.BlockSpec((rows, tn), lambda n: (0, n)),
        out_shape=jax.ShapeDtypeStruct((rows, n_out), F32),
        compiler_params=pltpu.CompilerParams(
            dimension_semantics=("arbitrary",), vmem_limit_bytes=2 * 2 * d * tn * 4 + 8 * MIB),
        name="ada",
    )(c_pad, w_ada, b_ada)


def _in_proj_kernel(x_ref, sh_ref, sc_ref, g_ref, w_ref, gq_ref, gk_ref, pos_ref, freq_ref,
                    qkv_ref, u_ref, *, tn):
    y = _rms_normalize(x_ref[...]) * g_ref[...]
    h = (y * (1.0 + sc_ref[0]) + sh_ref[0]).astype(BF16)

    ang = pos_ref[...].astype(F32) * freq_ref[...]
    lane = lax.broadcasted_iota(jnp.int32, ang.shape, 1)
    cos = jnp.cos(ang)
    sin = jnp.where(lane < HEAD_DIM // 2, -jnp.sin(ang), jnp.sin(ang))

    qkv_w = qkv_ref.shape[1]
    for n in range(w_ref.shape[1] // tn):
        z = jnp.dot(h, w_ref[:, n * tn:(n + 1) * tn], preferred_element_type=F32)
        col0 = n * tn
        if col0 < 2 * ATTN_WIDTH:
            g = gq_ref[...] if col0 < ATTN_WIDTH else gk_ref[...]
            for hh in range(tn // HEAD_DIM):
                zn = _rms_normalize(z[:, hh * HEAD_DIM:(hh + 1) * HEAD_DIM]) * g
                rot = zn * cos + pltpu.roll(zn, HEAD_DIM // 2, 1) * sin
                qkv_ref[:, col0 + hh * HEAD_DIM:col0 + (hh + 1) * HEAD_DIM] = rot.astype(BF16)
        elif col0 < qkv_w:
            qkv_ref[:, col0:col0 + tn] = z.astype(BF16)
        else:
            u_ref[:, col0 - qkv_w:col0 - qkv_w + tn] = z


def _in_proj(x2d, mod3, g_mix, w_in, g_q, g_k, pos, inv_freq, *, seq, tm, tn):
    t, d = x2d.shape
    n_out = w_in.shape[1]
    qkv_w = 3 * ATTN_WIDTH
    tiles_per_seq = seq // tm
    const = lambda shape: pl.BlockSpec(shape, lambda m: (0, 0))
    vmem = (2 * tm * d * 4 + d * n_out * 2 + 2 * tm * qkv_w * 2 + 2 * tm * POOL_WIDTH * 4
            + tm * d * 2 + 2 * tm * d * 4 + 4 * tm * tn * 4 + 4 * tm * 128 * 4)
    return pl.pallas_call(
        functools.partial(_in_proj_kernel, tn=tn),
        grid=(t // tm,),
        in_specs=[
            pl.BlockSpec((tm, d), lambda m: (m, 0)),
            pl.BlockSpec((1, 1, d), lambda m: ((m // tiles_per_seq) * N_MOD + 0, 0, 0)),
            pl.BlockSpec((1, 1, d), lambda m: ((m // tiles_per_seq) * N_MOD + 1, 0, 0)),
            const((1, d)),
            pl.BlockSpec((d, n_out), lambda m: (0, 0), pipeline_mode=pl.Buffered(1)),
            const((1, HEAD_DIM)),
            const((1, HEAD_DIM)),
            pl.BlockSpec((tm, 1), lambda m: (m, 0)),
            const((1, HEAD_DIM)),
        ],
        out_specs=[
            pl.BlockSpec((tm, qkv_w), lambda m: (m, 0)),
            pl.BlockSpec((tm, POOL_WIDTH), lambda m: (m, 0)),
        ],
        out_shape=[
            jax.ShapeDtypeStruct((t, qkv_w), BF16),
            jax.ShapeDtypeStruct((t, POOL_WIDTH), F32),
        ],
        compiler_params=pltpu.CompilerParams(
            dimension_semantics=("arbitrary",), vmem_limit_bytes=vmem + 6 * MIB),
        name="in_proj",
    )(x2d, mod3, mod3, g_mix, w_in, g_q, g_k, pos, inv_freq)


def _attn_kernel(q_ref, k_ref, v_ref, o_ref, vt_scr, kmean_scr, cpen_scr, *, n_blocks):
    blk = MOBA_BLOCK
    c = (HEAD_DIM ** -0.5) * math.log2(math.e)

    for j in range(n_blocks):
        rows = slice(j * blk, (j + 1) * blk)
        kmean_scr[j:j + 1, :] = jnp.sum(k_ref[rows, :].astype(F32), axis=0, keepdims=True) * (1.0 / blk)
        vt_scr[:, rows] = v_ref[rows, :].astype(F32).T.astype(BF16)
    km = kmean_scr[...]
    km_hi = km.astype(BF16)
    km_lo = (km - km_hi.astype(F32)).astype(BF16)
    km2 = jnp.concatenate([km_hi, km_lo], axis=0)

    kpos = lax.broadcasted_iota(jnp.int32, (blk, blk), 0)
    qpos = lax.broadcasted_iota(jnp.int32, (blk, blk), 1)
    cpen_scr[...] = jnp.where(kpos <= qpos, 0.0, -MASK_PENALTY)

    qt = 2 * blk
    blk_id = lax.broadcasted_iota(jnp.int32, (n_blocks, qt), 0)
    upper_half = (lax.broadcasted_iota(jnp.int32, (n_blocks, qt), 1) >= blk).astype(jnp.int32)

    def scores(a):
        return lax.dot_general(k_ref[0:(a + 1) * qt, :], q_ref[a * qt:(a + 1) * qt, :], NT_DIMS,
                               preferred_element_type=F32)

    def finish(a, s_all):
        g2 = lax.dot_general(km2, q_ref[a * qt:(a + 1) * qt, :], NT_DIMS, preferred_element_type=F32)
        gate = g2[:n_blocks] + g2[n_blocks:]
        past = blk_id < 2 * a + upper_half
        pens = []
        for j in range(2 * a + 1):
            gj = gate[j:j + 1, :]
            beats = past & ((gate > gj) | ((gate == gj) & (blk_id < j)))
            rank = jnp.sum(beats.astype(F32), axis=0, keepdims=True)
            pens.append(jnp.where(rank < MOBA_TOPK, 0.0, MASK_PENALTY))

        p_cols, l_cols = [], []
        for half in range(2):
            own = 2 * a + half
            lanes = slice(half * blk, (half + 1) * blk)
            blocks = [s_all[j * blk:(j + 1) * blk, lanes] for j in range(own)]
            pen = [pens[j][:, lanes] for j in range(own)]
            s_own = s_all[own * blk:(own + 1) * blk, lanes] + cpen_scr[...]
            m = jnp.max(s_own, axis=0, keepdims=True)
            for j in range(own):
                m = jnp.maximum(m, jnp.max(blocks[j], axis=0, keepdims=True) - pen[j])
            mc = m * c
            ps = [jnp.exp2(blocks[j] * c - (mc + pen[j] * c)) for j in range(own)]
            ps.append(jnp.exp2(s_own * c - mc))
            l_cols.append(functools.reduce(lambda x, y: x + y, [jnp.sum(p, axis=0, keepdims=True) for p in ps]))
            ps = [p.astype(BF16) for p in ps]
            if half == 0:
                ps.append(jnp.zeros((blk, blk), BF16))
            p_cols.append(jnp.concatenate(ps, axis=0))
        p_all = jnp.concatenate(p_cols, axis=1)
        l = jnp.concatenate(l_cols, axis=1)
        acc = jnp.dot(vt_scr[:, 0:(a + 1) * qt], p_all, preferred_element_type=F32)
        o_ref[a * qt:(a + 1) * qt, :] = (acc / l).T.astype(BF16)

    n_tiles = n_blocks // 2
    s_next = scores(0)
    for a in range(n_tiles):
        s_cur = s_next
        if a + 1 < n_tiles:
            s_next = scores(a + 1)
        finish(a, s_cur)


def _attention(qkv, *, batch, seq):
    t = qkv.shape[0]
    n_blocks = seq // MOBA_BLOCK
    h = N_ATTN_HEADS
    kern = functools.partial(_attn_kernel, n_blocks=n_blocks)
    return pl.pallas_call(
        kern,
        grid=(batch, h),
        in_specs=[
            pl.BlockSpec((seq, HEAD_DIM), lambda b, hh: (b, hh)),
            pl.BlockSpec((seq, HEAD_DIM), lambda b, hh: (b, h + hh)),
            pl.BlockSpec((seq, HEAD_DIM), lambda b, hh: (b, 2 * h + hh)),
        ],
        out_specs=pl.BlockSpec((seq, HEAD_DIM), lambda b, hh: (b, hh)),
        out_shape=jax.ShapeDtypeStruct((t, ATTN_WIDTH), BF16),
        scratch_shapes=[
            pltpu.VMEM((HEAD_DIM, seq), BF16),
            pltpu.VMEM((n_blocks, HEAD_DIM), F32),
            pltpu.VMEM((MOBA_BLOCK, MOBA_BLOCK), F32),
        ],
        compiler_params=pltpu.CompilerParams(
            dimension_semantics=("arbitrary", "arbitrary"), vmem_limit_bytes=32 * MIB),
        name="moba_attn",
    )(qkv, qkv, qkv)


def _out_proj_kernel(oa_ref, u_ref, halo_ref, wp_ref, ps_ref, w_ref, x_ref, gt_ref, sh_ref, sc_ref, g_ref,
                     x1_ref, h2_ref, ext_scr, *, seq, tm, sub):
    hw = MAX_POOL_WINDOW
    gw = POOL_GROUP_WIDTH
    n_sub = tm // sub

    y_attn = [jnp.dot(oa_ref[r * sub:(r + 1) * sub, :], w_ref[:ATTN_WIDTH, :], preferred_element_type=F32)
              for r in range(n_sub)]

    t0 = (pl.program_id(0) * tm) % seq
    halo = jnp.where(t0 == 0, 0.0, halo_ref[...])
    t = t0 + lax.broadcasted_iota(jnp.int32, (tm, 1), 0)
    pooled_out = []
    for g, w in enumerate(POOL_WINDOWS):
        cols = slice(g * gw, (g + 1) * gw)
        ug = u_ref[:, cols]
        ext_scr[0:hw, :] = halo[:, cols]
        ext_scr[hw:, :] = ug
        d = 1
        while 2 * d < w:
            ext_scr[8:, :] = ext_scr[8:, :] + ext_scr[8 - d:hw + tm - d, :]
            d *= 2
        win = ext_scr[hw:, :] + ext_scr[hw - d:hw + tm - d, :]
        cnt = jnp.minimum(t + 1, w).astype(F32)
        pooled = win / cnt - ug
        og = jnp.dot(pooled.astype(BF16), wp_ref[g], preferred_element_type=F32) * ps_ref[:, cols]
        pooled_out.append(og.astype(BF16))
    o_pool = jnp.concatenate(pooled_out, axis=1)

    gate = gt_ref[0]
    norm_scale = g_ref[...] * (1.0 + sc_ref[0])
    norm_shift = sh_ref[0]
    for r in range(n_sub):
        rows = slice(r * sub, (r + 1) * sub)
        y = y_attn[r] + jnp.dot(o_pool[rows, :], w_ref[ATTN_WIDTH:, :], preferred_element_type=F32)
        x1 = x_ref[rows, :] + gate * y
        x1_ref[rows, :] = x1
        h2_ref[rows, :] = (_rms_normalize(x1) * norm_scale + norm_shift).astype(BF16)


def _out_proj(o_attn, u, w_pool, pool_scale, w_out, x2d, mod3, g_ffn, *, seq, tm, sub):
    t, d = x2d.shape
    hw = MAX_POOL_WINDOW
    tiles_per_seq = seq // tm
    mod_spec = lambda i: pl.BlockSpec((1, 1, d), lambda m: ((m // tiles_per_seq) * N_MOD + i, 0, 0))
    vmem = (2 * tm * ATTN_WIDTH * 2 + 2 * tm * POOL_WIDTH * 4 + d * d * 2 + 4 * tm * d * 4 + 2 * tm * d * 2
            + (tm + hw) * POOL_GROUP_WIDTH * 4 + tm * POOL_WIDTH * 6 + 4 * sub * d * 4)
    return pl.pallas_call(
        functools.partial(_out_proj_kernel, seq=seq, tm=tm, sub=sub),
        grid=(t // tm,),
        in_specs=[
            pl.BlockSpec((tm, ATTN_WIDTH), lambda m: (m, 0)),
            pl.BlockSpec((tm, POOL_WIDTH), lambda m: (m, 0)),
            pl.BlockSpec((hw, POOL_WIDTH), lambda m: (jnp.maximum(m * (tm // hw) - 1, 0), 0)),
            pl.BlockSpec(w_pool.shape, lambda m: (0, 0, 0)),
            pl.BlockSpec((1, POOL_WIDTH), lambda m: (0, 0)),
            pl.BlockSpec((d, d), lambda m: (0, 0), pipeline_mode=pl.Buffered(1)),
            pl.BlockSpec((tm, d), lambda m: (m, 0)),
            mod_spec(2), mod_spec(3), mod_spec(4),
            pl.BlockSpec((1, d), lambda m: (0, 0)),
        ],
        out_specs=[
            pl.BlockSpec((tm, d), lambda m: (m, 0)),
            pl.BlockSpec((tm, d), lambda m: (m, 0)),
        ],
        out_shape=[
            jax.ShapeDtypeStruct((t, d), F32),
            jax.ShapeDtypeStruct((t, d), BF16),
        ],
        scratch_shapes=[pltpu.VMEM((tm + hw, POOL_GROUP_WIDTH), F32)],
        compiler_params=pltpu.CompilerParams(
            dimension_semantics=("arbitrary",), vmem_limit_bytes=vmem + 6 * MIB),
        name="out_proj",
    )(o_attn, u, u, w_pool, pool_scale, w_out, x2d, mod3, mod3, mod3, g_ffn)


def _ffn_kernel(h_ref, x_ref, gt_ref, wg_ref, wu_ref, wd_ref, o_ref):
    def partial_out():
        h = h_ref[...]
        a = jnp.dot(h, wg_ref[...], preferred_element_type=F32)
        b = jnp.dot(h, wu_ref[...], preferred_element_type=F32)
        act = (_silu(a) * b).astype(BF16)
        return gt_ref[0] * jnp.dot(act, wd_ref[...], preferred_element_type=F32)

    @pl.when(pl.program_id(1) == 0)
    def _():
        o_ref[...] = x_ref[...] + partial_out()

    @pl.when(pl.program_id(1) != 0)
    def _():
        o_ref[...] += partial_out()


def _ffn(h2, x1, mod3, w_gate, w_up, w_down, *, seq, tm, tf):
    t, d = x1.shape
    d_ff = w_gate.shape[1]
    tiles_per_seq = seq // tm
    vmem = 2 * tm * d * 2 + 4 * tm * d * 4 + 2 * 3 * d * tf * 2 + 4 * tm * tf * 4 + tm * d * 4
    return pl.pallas_call(
        _ffn_kernel,
        grid=(t // tm, d_ff // tf),
        in_specs=[
            pl.BlockSpec((tm, d), lambda m, f: (m, 0)),
            pl.BlockSpec((tm, d), lambda m, f: (m, 0)),
            pl.BlockSpec((1, 1, d), lambda m, f: ((m // tiles_per_seq) * N_MOD + 5, 0, 0)),
            pl.BlockSpec((d, tf), lambda m, f: (0, f)),
            pl.BlockSpec((d, tf), lambda m, f: (0, f)),
            pl.BlockSpec((tf, d), lambda m, f: (f, 0)),
        ],
        out_specs=pl.BlockSpec((tm, d), lambda m, f: (m, 0)),
        out_shape=jax.ShapeDtypeStruct((t, d), F32),
        compiler_params=pltpu.CompilerParams(
            dimension_semantics=("arbitrary", "arbitrary"), vmem_limit_bytes=vmem + 8 * MIB),
        name="ffn",
    )(h2, x1, mod3, w_gate, w_up, w_down)


def kernel(x, c, positions, w_ada, b_ada, g_mix_norm, w_in, g_q, g_k, w_pool, pool_scale, w_out,
           g_ffn_norm, w_gate, w_up, w_down):
    batch, seq, d = x.shape
    depth = w_ada.shape[0]
    t = batch * seq
    assert seq % (2 * MOBA_BLOCK) == 0 and d == ATTN_WIDTH + POOL_WIDTH
    assert w_in.shape[2] == 3 * ATTN_WIDTH + POOL_WIDTH

    half = jnp.arange(0, HEAD_DIM, 2, dtype=F32)
    inv_freq = ROPE_THETA ** (-half / HEAD_DIM)
    inv_freq = jnp.concatenate([inv_freq, inv_freq]).reshape(1, HEAD_DIM)
    pos = positions.reshape(t, 1)
    c_pad = jnp.zeros((8, d), F32).at[:batch].set(c)

    x2d = x.reshape(t, d)
    for l in range(depth):
        mod = _ada(c_pad, w_ada[l], b_ada[l].reshape(1, -1), tn=1024)[:batch]
        mod3 = mod.reshape(batch * N_MOD, 1, d)
        qkv, u = _in_proj(x2d, mod3, g_mix_norm[l].reshape(1, d), w_in[l].astype(BF16),
                          g_q[l].reshape(1, HEAD_DIM), g_k[l].reshape(1, HEAD_DIM), pos, inv_freq,
                          seq=seq, tm=512, tn=512)
        o_attn = _attention(qkv, batch=batch, seq=seq)
        x1, h2 = _out_proj(o_attn, u, w_pool[l].astype(BF16), pool_scale[l].reshape(1, POOL_WIDTH),
                           w_out[l].astype(BF16), x2d, mod3, g_ffn_norm[l].reshape(1, d), seq=seq, tm=512, sub=256)
        x2d = _ffn(h2, x1, mod3, w_gate[l].astype(BF16), w_up[l].astype(BF16), w_down[l].astype(BF16),
                   seq=seq, tm=512, tf=512)
    return x2d.reshape(batch, seq, d)
```

```python
import functools
import math

import jax
import jax.numpy as jnp
from jax import lax
from jax.experimental import pallas as pl
from jax.experimental.pallas import tpu as pltpu

F32 = jnp.float32
BF16 = jnp.bfloat16

N_ATTN_HEADS = 8
HEAD_DIM = 128
ATTN_WIDTH = N_ATTN_HEADS * HEAD_DIM
POOL_WINDOWS = (2, 4, 8, 16)
POOL_GROUP_WIDTH = 256
POOL_WIDTH = len(POOL_WINDOWS) * POOL_GROUP_WIDTH
MAX_POOL_WINDOW = max(POOL_WINDOWS)
MOBA_BLOCK = 256
MOBA_TOPK = 3
ROPE_THETA = 10000.0
N_MOD = 6
EPS = 1e-6
MASK_PENALTY = 1e30

MIB = 1024 * 1024
NT_DIMS = (((1,), (1,)), ((), ()))


def _rms_normalize(x):
    return x * lax.rsqrt(jnp.mean(x * x, axis=-1, keepdims=True) + EPS)


def _silu(a):
    return a * jax.nn.sigmoid(a)


def _ada_kernel(c_ref, w_ref, b_ref, o_ref):
    s = _silu(c_ref[...])
    o_ref[...] = jnp.dot(s.astype(BF16), w_ref[...].astype(BF16), preferred_element_type=F32) + b_ref[...]


def _ada(c_pad, w_ada, b_ada, tn):
    rows, d = c_pad.shape
    n_out = w_ada.shape[1]
    return pl.pallas_call(
        _ada_kernel,
        grid=(n_out // tn,),
        in_specs=[
            pl.BlockSpec((rows, d), lambda n: (0, 0)),
            pl.BlockSpec((d, tn), lambda n: (0, n)),
            pl.BlockSpec((1, tn), lambda n: (0, n)),
        ],
        out_specs=pl.BlockSpec((rows, tn), lambda n: (0, n)),
        out_shape=jax.ShapeDtypeStruct((rows, n_out), F32),
        compiler_params=pltpu.CompilerParams(
            dimension_semantics=("arbitrary",), vmem_limit_bytes=2 * 2 * d * tn * 4 + 8 * MIB),
        name="ada",
    )(c_pad, w_ada, b_ada)


def _pool_mixer(u, halo, t0, wp_ref, ps_ref, ext_scr):
    tm = u.shape[0]
    hw = MAX_POOL_WINDOW
    gw = POOL_GROUP_WIDTH
    t = t0 + lax.broadcasted_iota(jnp.int32, (tm, 1), 0)
    outs = []
    for g, w in enumerate(POOL_WINDOWS):
        cols = slice(g * gw, (g + 1) * gw)
        ug = u[:, cols]
        ext_scr[0:hw, :] = halo[:, cols]
        ext_scr[hw:, :] = ug
        d = 1
        while 2 * d < w:
            ext_scr[8:, :] = ext_scr[8:, :] + ext_scr[8 - d:hw + tm - d, :]
            d *= 2
        win = ext_scr[hw:, :] + ext_scr[hw - d:hw + tm - d, :]
        cnt = jnp.minimum(t + 1, w).astype(F32)
        pooled = win / cnt - ug
        og = jnp.dot(pooled.astype(BF16), wp_ref[g], preferred_element_type=F32) * ps_ref[:, cols]
        outs.append(og.astype(BF16))
    return outs


def _in_proj_kernel(x_ref, sh_ref, sc_ref, g_ref, w_ref, gq_ref, gk_ref, pos_ref, freq_ref, wp_ref, ps_ref,
                    qkv_ref, op_ref, ext_scr, tail_scr, *, tn, seq):
    tm = x_ref.shape[0]
    hw = MAX_POOL_WINDOW
    gw = POOL_GROUP_WIDTH
    qkv_w = qkv_ref.shape[1]
    m_idx = pl.program_id(0)

    @pl.when(m_idx == 0)
    def _():
        tail_scr[...] = jnp.zeros(tail_scr.shape, F32)

    y = _rms_normalize(x_ref[...]) * g_ref[...]
    h = (y * (1.0 + sc_ref[0]) + sh_ref[0]).astype(BF16)

    u = jnp.dot(h, w_ref[:, qkv_w:], preferred_element_type=F32)
    t0 = (m_idx * tm) % seq
    halo = jnp.where(t0 == 0, 0.0, tail_scr[...])
    tail_scr[...] = u[tm - hw:, :]
    for g, og in enumerate(_pool_mixer(u, halo, t0, wp_ref, ps_ref, ext_scr)):
        op_ref[:, g * gw:(g + 1) * gw] = og

    ang = pos_ref[...].astype(F32) * freq_ref[...]
    lane = lax.broadcasted_iota(jnp.int32, ang.shape, 1)
    cos = jnp.cos(ang)
    sin = jnp.where(lane < HEAD_DIM // 2, -jnp.sin(ang), jnp.sin(ang))

    for n in range(qkv_w // tn):
        z = jnp.dot(h, w_ref[:, n * tn:(n + 1) * tn], preferred_element_type=F32)
        col0 = n * tn
        if col0 < 2 * ATTN_WIDTH:
            g = gq_ref[...] if col0 < ATTN_WIDTH else gk_ref[...]
            for hh in range(tn // HEAD_DIM):
                zn = _rms_normalize(z[:, hh * HEAD_DIM:(hh + 1) * HEAD_DIM]) * g
                rot = zn * cos + pltpu.roll(zn, HEAD_DIM // 2, 1) * sin
                qkv_ref[:, col0 + hh * HEAD_DIM:col0 + (hh + 1) * HEAD_DIM] = rot.astype(BF16)
        else:
            qkv_ref[:, col0:col0 + tn] = z.astype(BF16)


def _in_proj(x2d, mod3, g_mix, w_in, g_q, g_k, pos, inv_freq, w_pool, pool_scale, *, seq, tm, tn):
    t, d = x2d.shape
    n_out = w_in.shape[1]
    qkv_w = 3 * ATTN_WIDTH
    tiles_per_seq = seq // tm
    const = lambda shape: pl.BlockSpec(shape, lambda m: (0,) * len(shape))
    vmem = (2 * tm * d * 4 + d * n_out * 2 + 2 * tm * qkv_w * 2 + 2 * tm * POOL_WIDTH * 2
            + tm * d * 2 + 2 * tm * d * 4 + 4 * tm * tn * 4 + 4 * tm * 128 * 4 + 3 * tm * POOL_WIDTH * 4)
    return pl.pallas_call(
        functools.partial(_in_proj_kernel, tn=tn, seq=seq),
        grid=(t // tm,),
        in_specs=[
            pl.BlockSpec((tm, d), lambda m: (m, 0)),
            pl.BlockSpec((1, 1, d), lambda m: ((m // tiles_per_seq) * N_MOD + 0, 0, 0)),
            pl.BlockSpec((1, 1, d), lambda m: ((m // tiles_per_seq) * N_MOD + 1, 0, 0)),
            const((1, d)),
            pl.BlockSpec((d, n_out), lambda m: (0, 0), pipeline_mode=pl.Buffered(1)),
            const((1, HEAD_DIM)),
            const((1, HEAD_DIM)),
            pl.BlockSpec((tm, 1), lambda m: (m, 0)),
            const((1, HEAD_DIM)),
            const(w_pool.shape),
            const((1, POOL_WIDTH)),
        ],
        out_specs=[
            pl.BlockSpec((tm, qkv_w), lambda m: (m, 0)),
            pl.BlockSpec((tm, POOL_WIDTH), lambda m: (m, 0)),
        ],
        out_shape=[
            jax.ShapeDtypeStruct((t, qkv_w), BF16),
            jax.ShapeDtypeStruct((t, POOL_WIDTH), BF16),
        ],
        scratch_shapes=[
            pltpu.VMEM((tm + MAX_POOL_WINDOW, POOL_GROUP_WIDTH), F32),
            pltpu.VMEM((MAX_POOL_WINDOW, POOL_WIDTH), F32),
        ],
        compiler_params=pltpu.CompilerParams(
            dimension_semantics=("arbitrary",), vmem_limit_bytes=vmem + 6 * MIB),
        name="in_proj",
    )(x2d, mod3, mod3, g_mix, w_in, g_q, g_k, pos, inv_freq, w_pool, pool_scale)


def _attn_kernel(q_ref, k_ref, v_ref, *rest, n_blocks, n_side):
    side_in, o_ref, side_out = rest[:n_side], rest[n_side], rest[n_side + 1:2 * n_side + 1]
    vt_scr, kmean_scr, cpen_scr = rest[2 * n_side + 1:]
    for w_in_ref, w_out_ref in zip(side_in, side_out):
        w_out_ref[...] = w_in_ref[...].astype(BF16)

    blk = MOBA_BLOCK
    c = (HEAD_DIM ** -0.5) * math.log2(math.e)

    for j in range(n_blocks):
        rows = slice(j * blk, (j + 1) * blk)
        kmean_scr[j:j + 1, :] = jnp.sum(k_ref[rows, :].astype(F32), axis=0, keepdims=True) * (1.0 / blk)
        vt_scr[:, rows] = v_ref[rows, :].astype(F32).T.astype(BF16)
    km = kmean_scr[...]
    km_hi = km.astype(BF16)
    km_lo = (km - km_hi.astype(F32)).astype(BF16)
    km2 = jnp.concatenate([km_hi, km_lo], axis=0)

    kpos = lax.broadcasted_iota(jnp.int32, (blk, blk), 0)
    qpos = lax.broadcasted_iota(jnp.int32, (blk, blk), 1)
    cpen_scr[...] = jnp.where(kpos <= qpos, 0.0, -MASK_PENALTY)

    qt = 2 * blk
    blk_id = lax.broadcasted_iota(jnp.int32, (n_blocks, qt), 0)
    upper_half = (lax.broadcasted_iota(jnp.int32, (n_blocks, qt), 1) >= blk).astype(jnp.int32)

    def scores(a):
        return lax.dot_general(k_ref[0:(a + 1) * qt, :], q_ref[a * qt:(a + 1) * qt, :], NT_DIMS,
                               preferred_element_type=F32)

    def finish(a, s_all):
        g2 = lax.dot_general(km2, q_ref[a * qt:(a + 1) * qt, :], NT_DIMS, preferred_element_type=F32)
        gate = g2[:n_blocks] + g2[n_blocks:]
        past = blk_id < 2 * a + upper_half
        pens = []
        for j in range(2 * a + 1):
            gj = gate[j:j + 1, :]
            beats = past & ((gate > gj) | ((gate == gj) & (blk_id < j)))
            rank = jnp.sum(beats.astype(F32), axis=0, keepdims=True)
            pens.append(jnp.where(rank < MOBA_TOPK, 0.0, MASK_PENALTY))

        p_cols, l_cols = [], []
        for half in range(2):
            own = 2 * a + half
            lanes = slice(half * blk, (half + 1) * blk)
            blocks = [s_all[j * blk:(j + 1) * blk, lanes] for j in range(own)]
            pen = [pens[j][:, lanes] for j in range(own)]
            s_own = s_all[own * blk:(own + 1) * blk, lanes] + cpen_scr[...]
            m = jnp.max(s_own, axis=0, keepdims=True)
            for j in range(own):
                m = jnp.maximum(m, jnp.max(blocks[j], axis=0, keepdims=True) - pen[j])
            mc = m * c
            ps = [jnp.exp2(blocks[j] * c - (mc + pen[j] * c)) for j in range(own)]
            ps.append(jnp.exp2(s_own * c - mc))
            l_cols.append(functools.reduce(lambda x, y: x + y, [jnp.sum(p, axis=0, keepdims=True) for p in ps]))
            ps = [p.astype(BF16) for p in ps]
            if half == 0:
                ps.append(jnp.zeros((blk, blk), BF16))
            p_cols.append(jnp.concatenate(ps, axis=0))
        p_all = jnp.concatenate(p_cols, axis=1)
        l = jnp.concatenate(l_cols, axis=1)
        acc = jnp.dot(vt_scr[:, 0:(a + 1) * qt], p_all, preferred_element_type=F32)
        o_ref[a * qt:(a + 1) * qt, :] = (acc / l).T.astype(BF16)

    n_tiles = n_blocks // 2
    s_next = scores(0)
    for a in range(n_tiles):
        s_cur = s_next
        if a + 1 < n_tiles:
            s_next = scores(a + 1)
        finish(a, s_cur)


def _attention(qkv, side_weights, *, batch, seq):
    t = qkv.shape[0]
    n_blocks = seq // MOBA_BLOCK
    h = N_ATTN_HEADS
    n_steps = batch * h
    side_specs = [pl.BlockSpec((w.shape[0] // n_steps, w.shape[1]), lambda b, hh: (b * h + hh, 0))
                  for w in side_weights]
    side_bytes = sum(2 * (w.size // n_steps) * 6 for w in side_weights)
    kern = functools.partial(_attn_kernel, n_blocks=n_blocks, n_side=len(side_weights))
    outs = pl.pallas_call(
        kern,
        grid=(batch, h),
        in_specs=[
            pl.BlockSpec((seq, HEAD_DIM), lambda b, hh: (b, hh)),
            pl.BlockSpec((seq, HEAD_DIM), lambda b, hh: (b, h + hh)),
            pl.BlockSpec((seq, HEAD_DIM), lambda b, hh: (b, 2 * h + hh)),
        ] + side_specs,
        out_specs=[pl.BlockSpec((seq, HEAD_DIM), lambda b, hh: (b, hh))] + side_specs,
        out_shape=[jax.ShapeDtypeStruct((t, ATTN_WIDTH), BF16)]
                  + [jax.ShapeDtypeStruct(w.shape, BF16) for w in side_weights],
        scratch_shapes=[
            pltpu.VMEM((HEAD_DIM, seq), BF16),
            pltpu.VMEM((n_blocks, HEAD_DIM), F32),
            pltpu.VMEM((MOBA_BLOCK, MOBA_BLOCK), F32),
        ],
        compiler_params=pltpu.CompilerParams(
            dimension_semantics=("arbitrary", "arbitrary"), vmem_limit_bytes=24 * MIB + side_bytes),
        name="moba_attn",
    )(qkv, qkv, qkv, *side_weights)
    return outs[0], outs[1:]


def _out_proj_kernel(oa_ref, op_ref, w_ref, x_ref, gt_ref, sh_ref, sc_ref, g_ref, x1_ref, h2_ref, *, sub):
    gate = gt_ref[0]
    norm_scale = g_ref[...] * (1.0 + sc_ref[0])
    norm_shift = sh_ref[0]
    n_sub = x_ref.shape[0] // sub

    def project(r):
        rows = slice(r * sub, (r + 1) * sub)
        return (jnp.dot(oa_ref[rows, :], w_ref[:ATTN_WIDTH, :], preferred_element_type=F32)
                + jnp.dot(op_ref[rows, :], w_ref[ATTN_WIDTH:, :], preferred_element_type=F32))

    y_next = project(0)
    for r in range(n_sub):
        rows = slice(r * sub, (r + 1) * sub)
        y = y_next
        if r + 1 < n_sub:
            y_next = project(r + 1)
        x1 = x_ref[rows, :] + gate * y
        x1_ref[rows, :] = x1
        h2_ref[rows, :] = (_rms_normalize(x1) * norm_scale + norm_shift).astype(BF16)


def _out_proj(o_attn, o_pool, w_out, x2d, mod3, g_ffn, *, seq, tm, sub):
    t, d = x2d.shape
    tiles_per_seq = seq // tm
    mod_spec = lambda i: pl.BlockSpec((1, 1, d), lambda m: ((m // tiles_per_seq) * N_MOD + i, 0, 0))
    vmem = 2 * tm * d * 2 + d * d * 2 + 4 * tm * d * 4 + 2 * tm * d * 2 + 6 * sub * d * 4
    return pl.pallas_call(
        functools.partial(_out_proj_kernel, sub=sub),
        grid=(t // tm,),
        in_specs=[
            pl.BlockSpec((tm, ATTN_WIDTH), lambda m: (m, 0)),
            pl.BlockSpec((tm, POOL_WIDTH), lambda m: (m, 0)),
            pl.BlockSpec((d, d), lambda m: (0, 0), pipeline_mode=pl.Buffered(1)),
            pl.BlockSpec((tm, d), lambda m: (m, 0)),
            mod_spec(2), mod_spec(3), mod_spec(4),
            pl.BlockSpec((1, d), lambda m: (0, 0)),
        ],
        out_specs=[
            pl.BlockSpec((tm, d), lambda m: (m, 0)),
            pl.BlockSpec((tm, d), lambda m: (m, 0)),
        ],
        out_shape=[
            jax.ShapeDtypeStruct((t, d), F32),
            jax.ShapeDtypeStruct((t, d), BF16),
        ],
        compiler_params=pltpu.CompilerParams(
            dimension_semantics=("arbitrary",), vmem_limit_bytes=vmem + 6 * MIB),
        name="out_proj",
    )(o_attn, o_pool, w_out, x2d, mod3, mod3, mod3, g_ffn)


def _ffn_kernel(h_ref, x_ref, gt_ref, wg_ref, wu_ref, wd_ref, o_ref):
    def partial_out():
        h = h_ref[...]
        a = jnp.dot(h, wg_ref[...], preferred_element_type=F32)
        b = jnp.dot(h, wu_ref[...], preferred_element_type=F32)
        act = (_silu(a) * b).astype(BF16)
        return gt_ref[0] * jnp.dot(act, wd_ref[...], preferred_element_type=F32)

    @pl.when(pl.program_id(1) == 0)
    def _():
        o_ref[...] = x_ref[...] + partial_out()

    @pl.when(pl.program_id(1) != 0)
    def _():
        o_ref[...] += partial_out()


def _ffn(h2, x1, mod3, w_gate, w_up, w_down, *, seq, tm, tf):
    t, d = x1.shape
    d_ff = w_gate.shape[1]
    tiles_per_seq = seq // tm
    vmem = 2 * tm * d * 2 + 4 * tm * d * 4 + 2 * 3 * d * tf * 2 + 4 * tm * tf * 4 + tm * d * 4
    return pl.pallas_call(
        _ffn_kernel,
        grid=(t // tm, d_ff // tf),
        in_specs=[
            pl.BlockSpec((tm, d), lambda m, f: (m, 0)),
            pl.BlockSpec((tm, d), lambda m, f: (m, 0)),
            pl.BlockSpec((1, 1, d), lambda m, f: ((m // tiles_per_seq) * N_MOD + 5, 0, 0)),
            pl.BlockSpec((d, tf), lambda m, f: (0, f)),
            pl.BlockSpec((d, tf), lambda m, f: (0, f)),
            pl.BlockSpec((tf, d), lambda m, f: (f, 0)),
        ],
        out_specs=pl.BlockSpec((tm, d), lambda m, f: (m, 0)),
        out_shape=jax.ShapeDtypeStruct((t, d), F32),
        compiler_params=pltpu.CompilerParams(
            dimension_semantics=("arbitrary", "arbitrary"), vmem_limit_bytes=vmem + 8 * MIB),
        name="ffn",
    )(h2, x1, mod3, w_gate, w_up, w_down)


def kernel(x, c, positions, w_ada, b_ada, g_mix_norm, w_in, g_q, g_k, w_pool, pool_scale, w_out,
           g_ffn_norm, w_gate, w_up, w_down):
    batch, seq, d = x.shape
    depth = w_ada.shape[0]
    t = batch * seq
    assert seq % (2 * MOBA_BLOCK) == 0 and d == ATTN_WIDTH + POOL_WIDTH
    assert w_in.shape[2] == 3 * ATTN_WIDTH + POOL_WIDTH

    half = jnp.arange(0, HEAD_DIM, 2, dtype=F32)
    inv_freq = ROPE_THETA ** (-half / HEAD_DIM)
    inv_freq = jnp.concatenate([inv_freq, inv_freq]).reshape(1, HEAD_DIM)
    pos = positions.reshape(t, 1)
    c_pad = jnp.zeros((8, d), F32).at[:batch].set(c)

    x2d = x.reshape(t, d)
    for l in range(depth):
        mod = _ada(c_pad, w_ada[l], b_ada[l].reshape(1, -1), tn=1024)[:batch]
        mod3 = mod.reshape(batch * N_MOD, 1, d)
        qkv, o_pool = _in_proj(x2d, mod3, g_mix_norm[l].reshape(1, d), w_in[l].astype(BF16),
                               g_q[l].reshape(1, HEAD_DIM), g_k[l].reshape(1, HEAD_DIM), pos, inv_freq,
                               w_pool[l].astype(BF16), pool_scale[l].reshape(1, POOL_WIDTH),
                               seq=seq, tm=512, tn=512)
        o_attn, (w_out_b, w_gate_b, w_up_b, w_down_b) = _attention(
            qkv, (w_out[l], w_gate[l], w_up[l], w_down[l]), batch=batch, seq=seq)
        x1, h2 = _out_proj(o_attn, o_pool, w_out_b, x2d, mod3, g_ffn_norm[l].reshape(1, d),
                           seq=seq, tm=512, sub=256)
        x2d = _ffn(h2, x1, mod3, w_gate_b, w_up_b, w_down_b, seq=seq, tm=512, tf=512)
    return x2d.reshape(batch, seq, d)
```
